```python
import jax, jax.numpy as jnp
from jax import lax
import numpy as np

D_MODEL = 1024
BATCH = 4
SEQ = 8192
DEPTH = 1

CONV_DIM = 512
CONV_KERNEL = 31
HGRN_DIM = 1024
HGRN_HEADS = 8
HGRN_HEAD_DIM = HGRN_DIM // HGRN_HEADS
HGRN_CHUNK = 64
N_BRANCHES = 2
D_FF = 2816
FFN_KERNEL = 3
LN_EPS = 1e-5
RMS_EPS = 1e-6
ALPHA = (2.0 * DEPTH) ** 0.25
BETA = (8.0 * DEPTH) ** -0.25

IN_SPLITS = [CONV_DIM, CONV_DIM, HGRN_DIM, HGRN_DIM, HGRN_DIM, HGRN_DIM, N_BRANCHES * D_MODEL]
IN_COLS = sum(IN_SPLITS)
IN_OFFSETS = list(np.cumsum(IN_SPLITS)[:-1])

kernel_name = "hybrid_conformer_conv_hgrn2_gated_merge_convffn"


def layer_norm(x, g, b):
    xf = x.astype(jnp.float32)
    mu = jnp.mean(xf, axis=-1, keepdims=True)
    var = jnp.mean(jnp.square(xf - mu), axis=-1, keepdims=True)
    y = (xf - mu) * lax.rsqrt(var + LN_EPS) * g.astype(jnp.float32) + b.astype(jnp.float32)
    return y.astype(x.dtype)


def causal_dwconv(x, w, b):
    k_w = w.shape[0]
    c = x.shape[-1]
    y = lax.conv_general_dilated(
        x, w[:, None, :].astype(x.dtype), window_strides=(1,), padding=[(k_w - 1, 0)],
        dimension_numbers=("NWC", "WIO", "NWC"), feature_group_count=c)
    return y + b.astype(x.dtype)


def hgrn2_chunked(q, k, v, logf):
    bsz, seq, nh, dk = q.shape
    dv = v.shape[-1]
    nc = seq // HGRN_CHUNK

    def to_chunks(t):
        return t.reshape(bsz, nc, HGRN_CHUNK, nh, t.shape[-1]).transpose(1, 0, 3, 2, 4)

    qc, kc, vc, lfc = to_chunks(q), to_chunks(k), to_chunks(v), to_chunks(logf)
    bc = jnp.cumsum(lfc, axis=3)
    mask = jnp.tril(jnp.ones((HGRN_CHUNK, HGRN_CHUNK), dtype=bool))[:, :, None]

    def step(state, inp):
        q_c, k_c, v_c, b_c = inp
        diff = b_c[:, :, :, None, :] - b_c[:, :, None, :, :]
        decay = jnp.exp(jnp.where(mask, diff, -jnp.inf))
        scores = jnp.einsum("bhtk,bhsk,bhtsk->bhts", q_c, k_c, decay)
        o_intra = jnp.einsum("bhts,bhsv->bhtv", scores, v_c)
        o_inter = jnp.einsum("bhtk,bhkv->bhtv", q_c * jnp.exp(b_c), state)
        b_last = b_c[:, :, -1, :]
        k_tail = k_c * jnp.exp(b_last[:, :, None, :] - b_c)
        new_state = jnp.exp(b_last)[..., None] * state + jnp.einsum("bhsk,bhsv->bhkv", k_tail, v_c)
        return new_state, o_intra + o_inter

    s0 = jnp.zeros((bsz, nh, dk, dv), jnp.float32)
    _, oc = lax.scan(step, s0, (qc, kc, vc, bc))
    return oc.transpose(1, 0, 3, 2, 4).reshape(bsz, seq, nh, dv)


def setup_inputs(seed: int = 0) -> dict:
    key = jax.random.key(seed)
    ks = jax.random.split(key, 20)

    def nrm(k, shape, scale):
        return jax.random.normal(k, shape, jnp.float32) * scale

    col_scale = jnp.concatenate([
        jnp.ones((2 * CONV_DIM + 2 * HGRN_DIM,), jnp.float32),
        jnp.full((HGRN_DIM,), BETA, jnp.float32),
        jnp.ones((HGRN_DIM + N_BRANCHES * D_MODEL,), jnp.float32)])
    ffn_scale = jnp.concatenate([jnp.full((D_FF,), BETA, jnp.float32), jnp.ones((D_FF,), jnp.float32)])
    return {
        "x": nrm(ks[0], (BATCH, SEQ, D_MODEL), 1.0),
        "w_in": nrm(ks[1], (DEPTH, D_MODEL, IN_COLS), D_MODEL ** -0.5) * col_scale,
        "w_conv_dw": nrm(ks[2], (DEPTH, CONV_KERNEL, CONV_DIM), CONV_KERNEL ** -0.5),
        "b_conv_dw": nrm(ks[3], (DEPTH, CONV_DIM), 0.02),
        "conv_ln_g": 1.0 + nrm(ks[4], (DEPTH, CONV_DIM), 0.02),
        "conv_ln_b": nrm(ks[5], (DEPTH, CONV_DIM), 0.02),
        "w_conv_out": nrm(ks[6], (DEPTH, CONV_DIM, D_MODEL), BETA * CONV_DIM ** -0.5),
        "hgrn_lb_logits": nrm(ks[7], (DEPTH + 1, HGRN_DIM), 0.5),
        "hgrn_norm_g": 1.0 + nrm(ks[8], (DEPTH, HGRN_DIM), 0.02),
        "w_hgrn_out": nrm(ks[9], (DEPTH, HGRN_DIM, D_MODEL), BETA * HGRN_DIM ** -0.5),
        "w_out": nrm(ks[10], (DEPTH, D_MODEL, D_MODEL), BETA * D_MODEL ** -0.5),
        "ln1_g": 1.0 + nrm(ks[11], (DEPTH, D_MODEL), 0.02),
        "ln1_b": nrm(ks[12], (DEPTH, D_MODEL), 0.02),
        "w_ffn_in": nrm(ks[13], (DEPTH, D_MODEL, 2 * D_FF), D_MODEL ** -0.5) * ffn_scale,
        "w_ffn_dw": nrm(ks[14], (DEPTH, FFN_KERNEL, D_FF), FFN_KERNEL ** -0.5),
        "b_ffn_dw": nrm(ks[15], (DEPTH, D_FF), 0.02),
        "w_ffn_out": nrm(ks[16], (DEPTH, D_FF, D_MODEL), BETA * D_FF ** -0.5),
        "ln2_g": 1.0 + nrm(ks[17], (DEPTH, D_MODEL), 0.02),
        "ln2_b": nrm(ks[18], (DEPTH, D_MODEL), 0.02),
    }


def reference(x, w_in, w_conv_dw, b_conv_dw, conv_ln_g, conv_ln_b, w_conv_out,
              hgrn_lb_logits, hgrn_norm_g, w_hgrn_out, w_out, ln1_g, ln1_b,
              w_ffn_in, w_ffn_dw, b_ffn_dw, w_ffn_out, ln2_g, ln2_b):
    bsz, seq, _ = x.shape
    lb_all = jnp.cumsum(jax.nn.softmax(hgrn_lb_logits.astype(jnp.float32), axis=0), axis=0)
    for l in range(DEPTH):
        h = x
        proj = h @ w_in[l]
        c_val, c_gate, q_z, f_z, i_v, g_z, m_z = jnp.split(proj, IN_OFFSETS, axis=-1)

        c = c_val * jax.nn.sigmoid(c_gate)
        c = causal_dwconv(c, w_conv_dw[l], b_conv_dw[l])
        c = jax.nn.silu(layer_norm(c, conv_ln_g[l], conv_ln_b[l]))
        y_conv = c @ w_conv_out[l]

        lb = lb_all[l]
        zf = f_z.astype(jnp.float32)
        logf = jnp.log(lb + (1.0 - lb) * jax.nn.sigmoid(zf))
        k_in = (1.0 - lb) * jax.nn.sigmoid(-zf)
        qf = jax.nn.silu(q_z.astype(jnp.float32))
        heads = lambda t: t.reshape(bsz, seq, HGRN_HEADS, HGRN_HEAD_DIM)
        o = hgrn2_chunked(heads(qf), heads(k_in), heads(i_v.astype(jnp.float32)), heads(logf))
        o = o * lax.rsqrt(jnp.mean(jnp.square(o), axis=-1, keepdims=True) + RMS_EPS)
        o = o.reshape(bsz, seq, HGRN_DIM) * hgrn_norm_g[l].astype(jnp.float32)
        o = o.astype(x.dtype) * jax.nn.silu(g_z)
        y_hgrn = o @ w_hgrn_out[l]

        gates = jax.nn.sigmoid(m_z).reshape(bsz, seq, N_BRANCHES, D_MODEL)
        mixed = gates[:, :, 0, :] * y_conv + gates[:, :, 1, :] * y_hgrn
        mix = mixed @ w_out[l]
        x = layer_norm(ALPHA * x + mix, ln1_g[l], ln1_b[l])

        z = x @ w_ffn_in[l]
        u, gv = jnp.split(z, [D_FF], axis=-1)
        u = causal_dwconv(u, w_ffn_dw[l], b_ffn_dw[l])
        y_ffn = (jax.nn.gelu(u) * gv) @ w_ffn_out[l]
        x = layer_norm(ALPHA * x + y_ffn, ln2_g[l], ln2_b[l])
    return x
```

```python
import functools

import numpy as np
import jax
import jax.numpy as jnp
from jax import lax
from jax.experimental import pallas as pl
from jax.experimental.pallas import tpu as pltpu

F32 = jnp.float32
BF16 = jnp.bfloat16

CONV_DIM = 512
CONV_KERNEL = 31
HGRN_DIM = 1024
HGRN_HEADS = 8
HEAD_DIM = HGRN_DIM // HGRN_HEADS
FFN_KERNEL = 3
LN_EPS = 1e-5
RMS_EPS = 1e-6

OFF_CVAL = 0
OFF_CGATE = CONV_DIM
OFF_Q = 2 * CONV_DIM
OFF_F = OFF_Q + HGRN_DIM
OFF_I = OFF_F + HGRN_DIM
OFF_G = OFF_I + HGRN_DIM
OFF_M = OFF_G + HGRN_DIM

SUBLANES = 8
BF16_ROWS = 16
TOKEN_TILE = 256
CHUNK = 128
ROW_BLOCK = 16
CONV_HIST = 32
FFN_HIST = 8
LEVEL_DIAG = 100
VMEM_LIMIT_BYTES = 56 * 1024 * 1024


def _sig(x):
    return 0.5 * jnp.tanh(0.5 * x) + 0.5


def _dot(a, b):
    return jnp.dot(a, b, preferred_element_type=F32)


def _dot_nt(a, b):
    return lax.dot_general(a, b, (((1,), (1,)), ((), ())), preferred_element_type=F32)


def _layer_norm(x, g, b):
    mu = jnp.mean(x, axis=-1, keepdims=True)
    xc = x - mu
    var = jnp.mean(xc * xc, axis=-1, keepdims=True)
    return xc * lax.rsqrt(var + LN_EPS) * g + b


def _levels(chunk):
    out, m = [], chunk // 2
    while m >= 1:
        out.append(m)
        m //= 2
    return out


def _level_matrix(chunk):
    t = np.arange(chunk)[:, None]
    s = np.arange(chunk)[None, :]
    hb = np.floor(np.log2(np.maximum(t ^ s, 1))).astype(np.int64)
    li = int(np.log2(chunk)) - 1 - hb
    return np.where(t > s, li, np.where(t == s, LEVEL_DIAG, -1)).astype(np.int32)


def _mixer_kernel(x_ref, w_in_ref, wdw_ref, bdw_ref, clg_ref, clb_ref, wco_ref, lbl_ref, hng_ref,
                  who_ref, wout_ref, l1g_ref, l1b_ref, tril_ref, lvl_ref,
                  o_ref,
                  proj_ref, cbuf_ref, csh_ref, cn_ref, yc_ref, ls_ref, qf_ref, kk_ref, b_ref, a_ref,
                  qi_ref, kt_ref, qb_ref, kb_ref, vb_ref, on_ref, og_ref, yh_ref, mx_ref, st_ref,
                  *, layer, alpha):
    j = pl.program_id(1)
    ts = x_ref.shape[0]
    chunk = b_ref.shape[0]
    levels = _levels(chunk)

    @pl.when(j == 0)
    def _():
        cbuf_ref[0:CONV_HIST, :] = jnp.zeros((CONV_HIST, CONV_DIM), F32)
        st_ref[...] = jnp.zeros(st_ref.shape, F32)

    @pl.when(j != 0)
    def _():
        cbuf_ref[0:CONV_HIST, :] = cbuf_ref[ts:ts + CONV_HIST, :]

    proj_ref[...] = _dot(x_ref[...].astype(BF16), w_in_ref[...])

    cbuf_ref[CONV_HIST:CONV_HIST + ts, :] = (
        proj_ref[:, OFF_CVAL:OFF_CVAL + CONV_DIM] * _sig(proj_ref[:, OFF_CGATE:OFF_CGATE + CONV_DIM]))
    n_sh = csh_ref.shape[1]
    for r in range(1, SUBLANES):
        csh_ref[r - 1] = cbuf_ref[r:r + n_sh, :]

    def conv_block(i, carry):
        t0 = pl.multiple_of(i * ROW_BLOCK, ROW_BLOCK)
        acc = jnp.broadcast_to(bdw_ref[...], (ROW_BLOCK, CONV_DIM))
        for k in range(CONV_KERNEL):
            a, r = divmod(k + CONV_HIST - (CONV_KERNEL - 1), SUBLANES)
            start = pl.multiple_of(t0 + SUBLANES * a, SUBLANES)
            if r == 0:
                win = cbuf_ref[pl.ds(start, ROW_BLOCK), :]
            else:
                win = csh_ref[r - 1, pl.ds(start, ROW_BLOCK), :]
            acc = acc + wdw_ref[k:k + 1, :] * win
        y = _layer_norm(acc, clg_ref[...], clb_ref[...])
        cn_ref[pl.ds(t0, ROW_BLOCK), :] = (y * _sig(y)).astype(BF16)
        return carry

    lax.fori_loop(0, ts // ROW_BLOCK, conv_block, 0)
    yc_ref[...] = _dot(cn_ref[...], wco_ref[...])

    lg = lbl_ref[...]
    e = jnp.exp(lg - jnp.max(lg, axis=0, keepdims=True))
    lb = jnp.sum(e[0:layer + 1], axis=0, keepdims=True) / jnp.sum(e, axis=0, keepdims=True)
    oml = 1.0 - lb

    def prep_block(i, carry):
        t0 = pl.multiple_of(i * ROW_BLOCK, ROW_BLOCK)
        rows = pl.ds(t0, ROW_BLOCK)
        th = jnp.tanh(0.5 * proj_ref[rows, OFF_F:OFF_F + HGRN_DIM])
        f = lb + oml * (0.5 + 0.5 * th)
        kk_ref[rows, :] = oml * (0.5 - 0.5 * th)
        logf = jnp.log(f)
        hi = logf.astype(BF16)
        r1 = logf - hi.astype(F32)
        mid = r1.astype(BF16)
        ls_ref[0, rows, :] = hi
        ls_ref[1, rows, :] = mid
        ls_ref[2, rows, :] = (r1 - mid.astype(F32)).astype(BF16)
        q = proj_ref[rows, OFF_Q:OFF_Q + HGRN_DIM]
        qf_ref[rows, :] = q * _sig(q)
        return carry

    lax.fori_loop(0, ts // ROW_BLOCK, prep_block, 0)

    sub = lax.broadcasted_iota(jnp.int32, (SUBLANES, HGRN_DIM), 0)

    def chunk_body(c, carry):
        r0 = pl.multiple_of(c * chunk, chunk)
        rows = pl.ds(r0, chunk)
        tril = tril_ref[...]
        b_ref[...] = (_dot(tril, ls_ref[0, rows, :]) + _dot(tril, ls_ref[1, rows, :])
                      + _dot(tril, ls_ref[2, rows, :]))

        for li, m in enumerate(levels):
            if m >= SUBLANES:
                for p in range(chunk // (2 * m)):
                    base = p * 2 * m
                    bref = b_ref[base + m - 1:base + m, :]
                    left = kk_ref[pl.ds(r0 + base, m), :] * jnp.exp(bref - b_ref[base:base + m, :])
                    right = (qf_ref[pl.ds(r0 + base + m, m), :]
                             * jnp.exp(b_ref[base + m:base + 2 * m, :] - bref))
                    a_ref[li, base:base + 2 * m, :] = jnp.concatenate([left, right], axis=0).astype(BF16)
            elif m > 1:
                is_right = (sub & m) != 0
                for g2 in range(chunk // BF16_ROWS):
                    parts = []
                    for g in (2 * g2, 2 * g2 + 1):
                        gb = g * SUBLANES
                        if m == 4:
                            bref = jnp.broadcast_to(b_ref[gb + 3:gb + 4, :], (SUBLANES, HGRN_DIM))
                        else:
                            bref = jnp.where(
                                sub < 4,
                                jnp.broadcast_to(b_ref[gb + 1:gb + 2, :], (SUBLANES, HGRN_DIM)),
                                jnp.broadcast_to(b_ref[gb + 5:gb + 6, :], (SUBLANES, HGRN_DIM)))
                        d = b_ref[gb:gb + SUBLANES, :] - bref
                        ex = jnp.exp(jnp.where(is_right, d, -d))
                        src = jnp.where(is_right, qf_ref[pl.ds(r0 + gb, SUBLANES), :],
                                        kk_ref[pl.ds(r0 + gb, SUBLANES), :])
                        parts.append(src * ex)
                    a_ref[li, g2 * BF16_ROWS:(g2 + 1) * BF16_ROWS, :] = (
                        jnp.concatenate(parts, axis=0).astype(BF16))
            else:
                is_right = (sub & 1) != 0
                for g2 in range(chunk // BF16_ROWS):
                    parts = []
                    for g in (2 * g2, 2 * g2 + 1):
                        gb = g * SUBLANES
                        kk = kk_ref[pl.ds(r0 + gb, SUBLANES), :]
                        parts.append(jnp.where(is_right, qf_ref[pl.ds(r0 + gb, SUBLANES), :] * (1.0 - kk), kk))
                    a_ref[li, g2 * BF16_ROWS:(g2 + 1) * BF16_ROWS, :] = (
                        jnp.concatenate(parts, axis=0).astype(BF16))

        b_last = b_ref[chunk - 1:chunk, :]
        eb_last = jnp.exp(b_last)
        for rb in range(chunk // BF16_ROWS):
            rs = slice(rb * BF16_ROWS, (rb + 1) * BF16_ROWS)
            rd = pl.ds(r0 + rb * BF16_ROWS, BF16_ROWS)
            bb = b_ref[rs, :]
            qv = qf_ref[rd, :]
            kv = kk_ref[rd, :]
            qi_ref[rs, :] = (qv * jnp.exp(bb)).astype(BF16)
            kt_ref[rs, :] = (kv * jnp.exp(b_last - bb)).astype(BF16)
            qb_ref[rs, :] = qv.astype(BF16)
            kb_ref[rs, :] = kv.astype(BF16)
            vb_ref[rs, :] = proj_ref[rd, OFF_I:OFF_I + HGRN_DIM].astype(BF16)

        lvl = lvl_ref[...]
        for h in range(HGRN_HEADS):
            hs = slice(h * HEAD_DIM, (h + 1) * HEAD_DIM)
            sc = jnp.where(lvl == LEVEL_DIAG, _dot_nt(qb_ref[:, hs], kb_ref[:, hs]), 0.0)
            for li in range(len(levels)):
                a = a_ref[li, :, hs]
                sc = jnp.where(lvl == li, _dot_nt(a, a), sc)
            st = st_ref[h]
            o = _dot(sc.astype(BF16), vb_ref[:, hs]) + _dot_nt(qi_ref[:, hs], st.astype(BF16))
            v_t = proj_ref[rows, OFF_I + h * HEAD_DIM:OFF_I + (h + 1) * HEAD_DIM].T.astype(BF16)
            st_ref[h] = st * eb_last[:, hs] + _dot(v_t, kt_ref[:, hs])
            ms = jnp.mean(o * o, axis=-1, keepdims=True)
            on_ref[rows, hs] = o * lax.rsqrt(ms + RMS_EPS)
        return carry

    lax.fori_loop(0, ts // chunk, chunk_body, 0)

    def gate_block(i, carry):
        t0 = pl.multiple_of(i * ROW_BLOCK, ROW_BLOCK)
        rows = pl.ds(t0, ROW_BLOCK)
        gz = proj_ref[rows, OFF_G:OFF_G + HGRN_DIM]
        og_ref[rows, :] = (on_ref[rows, :] * hng_ref[...] * (gz * _sig(gz))).astype(BF16)
        return carry

    lax.fori_loop(0, ts // ROW_BLOCK, gate_block, 0)
    yh_ref[...] = _dot(og_ref[...], who_ref[...])

    def merge_block(i, carry):
        t0 = pl.multiple_of(i * ROW_BLOCK, ROW_BLOCK)
        rows = pl.ds(t0, ROW_BLOCK)
        d_model = yc_ref.shape[1]
        ga = _sig(proj_ref[rows, OFF_M:OFF_M + d_model])
        gb = _sig(proj_ref[rows, OFF_M + d_model:OFF_M + 2 * d_model])
        mx_ref[rows, :] = (ga * yc_ref[rows, :] + gb * yh_ref[rows, :]).astype(BF16)
        return carry

    lax.fori_loop(0, ts // ROW_BLOCK, merge_block, 0)
    yh_ref[...] = _dot(mx_ref[...], wout_ref[...])

    def norm_block(i, carry):
        t0 = pl.multiple_of(i * ROW_BLOCK, ROW_BLOCK)
        rows = pl.ds(t0, ROW_BLOCK)
        o_ref[rows, :] = _layer_norm(alpha * x_ref[rows, :] + yh_ref[rows, :], l1g_ref[...], l1b_ref[...])
        return carry

    lax.fori_loop(0, ts // ROW_BLOCK, norm_block, 0)


def _resident(shape):
    nd = len(shape)
    return pl.BlockSpec(shape, lambda b, j: (0,) * nd, pipeline_mode=pl.Buffered(1))


def _mixer(x, w_in, wdw, bdw, clg, clb, wco, lbl, hng, who, wout, l1g, l1b, *, layer, alpha):
    bsz, seq, d_model = x.shape
    ts, chunk = TOKEN_TILE, CHUNK
    assert seq % ts == 0 and ts % chunk == 0 and d_model == HGRN_DIM
    n_lev = len(_levels(chunk))
    tril = jnp.asarray(np.tril(np.ones((chunk, chunk), np.float32)), BF16)
    lvl = jnp.asarray(_level_matrix(chunk))
    consts = (w_in, wdw, bdw, clg, clb, wco, lbl, hng, who, wout, l1g, l1b, tril, lvl)
    tile = pl.BlockSpec((None, ts, d_model), lambda b, j: (b, j, 0))
    scratch = [
        pltpu.VMEM((ts, w_in.shape[1]), F32),
        pltpu.VMEM((ts + CONV_HIST, CONV_DIM), F32),
        pltpu.VMEM((SUBLANES - 1, ts + CONV_HIST - SUBLANES, CONV_DIM), F32),
        pltpu.VMEM((ts, CONV_DIM), BF16),
        pltpu.VMEM((ts, d_model), F32),
        pltpu.VMEM((3, ts, HGRN_DIM), BF16),
        pltpu.VMEM((ts, HGRN_DIM), F32),
        pltpu.VMEM((ts, HGRN_DIM), F32),
        pltpu.VMEM((chunk, HGRN_DIM), F32),
        pltpu.VMEM((n_lev, chunk, HGRN_DIM), BF16),
        pltpu.VMEM((chunk, HGRN_DIM), BF16),
        pltpu.VMEM((chunk, HGRN_DIM), BF16),
        pltpu.VMEM((chunk, HGRN_DIM), BF16),
        pltpu.VMEM((chunk, HGRN_DIM), BF16),
        pltpu.VMEM((chunk, HGRN_DIM), BF16),
        pltpu.VMEM((ts, HGRN_DIM), F32),
        pltpu.VMEM((ts, HGRN_DIM), BF16),
        pltpu.VMEM((ts, d_model), F32),
        pltpu.VMEM((ts, d_model), BF16),
        pltpu.VMEM((HGRN_HEADS, HEAD_DIM, HEAD_DIM), F32),
    ]
    return pl.pallas_call(
        functools.partial(_mixer_kernel, layer=layer, alpha=alpha),
        grid=(bsz, seq // ts),
        in_specs=[tile] + [_resident(c.shape) for c in consts],
        out_specs=tile,
        out_shape=jax.ShapeDtypeStruct(x.shape, x.dtype),
        scratch_shapes=scratch,
        compiler_params=pltpu.CompilerParams(
            dimension_semantics=("arbitrary", "arbitrary"), vmem_limit_bytes=VMEM_LIMIT_BYTES),
        name="mixer",
    )(x, *consts)


def _ffn_kernel(x_ref, wfi_ref, wdw_ref, bdw_ref, wfo_ref, g_ref, b_ref, o_ref,
                ubuf_ref, gv_ref, h_ref, y_ref, *, alpha):
    j = pl.program_id(1)
    ts = x_ref.shape[0]
    d_ff = gv_ref.shape[1]

    @pl.when(j == 0)
    def _():
        ubuf_ref[0:FFN_HIST, :] = jnp.zeros((FFN_HIST, d_ff), F32)

    @pl.when(j != 0)
    def _():
        ubuf_ref[0:FFN_HIST, :] = ubuf_ref[ts:ts + FFN_HIST, :]

    xb = x_ref[...].astype(BF16)
    ubuf_ref[FFN_HIST:FFN_HIST + ts, :] = _dot(xb, wfi_ref[:, 0:d_ff])
    gv_ref[...] = _dot(xb, wfi_ref[:, d_ff:2 * d_ff])

    def act_block(i, carry):
        t0 = pl.multiple_of(i * ROW_BLOCK, ROW_BLOCK)
        win = ubuf_ref[pl.ds(t0, ROW_BLOCK + FFN_HIST), :]
        u = bdw_ref[...] + wdw_ref[FFN_KERNEL - 1:FFN_KERNEL, :] * win[FFN_HIST:, :]
        for k in range(1, FFN_KERNEL):
            shifted = pltpu.roll(win, k, 0)[FFN_HIST:, :]
            u = u + wdw_ref[FFN_KERNEL - 1 - k:FFN_KERNEL - k, :] * shifted
        gelu = 0.5 * u * (1.0 + jnp.tanh(0.7978845608028654 * (u + 0.044715 * (u * u * u))))
        h_ref[pl.ds(t0, ROW_BLOCK), :] = (gelu * gv_ref[pl.ds(t0, ROW_BLOCK), :]).astype(BF16)
        return carry

    lax.fori_loop(0, ts // ROW_BLOCK, act_block, 0)
    y_ref[...] = _dot(h_ref[...], wfo_ref[...])

    def norm_block(i, carry):
        t0 = pl.multiple_of(i * ROW_BLOCK, ROW_BLOCK)
        rows = pl.ds(t0, ROW_BLOCK)
        o_ref[rows, :] = _layer_norm(alpha * x_ref[rows, :] + y_ref[rows, :], g_ref[...], b_ref[...])
        return carry

    lax.fori_loop(0, ts // ROW_BLOCK, norm_block, 0)


def _ffn(x, wfi, wdw, bdw, wfo, g, b, *, alpha):
    bsz, seq, d_model = x.shape
    ts = TOKEN_TILE
    d_ff = wfo.shape[0]
    assert seq % ts == 0
    consts = (wfi, wdw, bdw, wfo, g, b)
    tile = pl.BlockSpec((None, ts, d_model), lambda b_, j: (b_, j, 0))
    scratch = [
        pltpu.VMEM((ts + FFN_HIST, d_ff), F32),
        pltpu.VMEM((ts, d_ff), F32),
        pltpu.VMEM((ts, d_ff), BF16),
        pltpu.VMEM((ts, d_model), F32),
    ]
    return pl.pallas_call(
        functools.partial(_ffn_kernel, alpha=alpha),
        grid=(bsz, seq // ts),
        in_specs=[tile] + [_resident(c.shape) for c in consts],
        out_specs=tile,
        out_shape=jax.ShapeDtypeStruct(x.shape, x.dtype),
        scratch_shapes=scratch,
        compiler_params=pltpu.CompilerParams(
            dimension_semantics=("arbitrary", "arbitrary"), vmem_limit_bytes=VMEM_LIMIT_BYTES),
        name="ffn",
    )(x, *consts)


def kernel(x, w_in, w_conv_dw, b_conv_dw, conv_ln_g, conv_ln_b, w_conv_out, hgrn_lb_logits, hgrn_norm_g,
           w_hgrn_out, w_out, ln1_g, ln1_b, w_ffn_in, w_ffn_dw, b_ffn_dw, w_ffn_out, ln2_g, ln2_b):
    depth = w_in.shape[0]
    alpha = (2.0 * depth) ** 0.25
    row = lambda v: v.reshape(1, -1).astype(F32)
    for l in range(depth):
        x = _mixer(
            x, w_in[l].astype(BF16), w_conv_dw[l].astype(F32), row(b_conv_dw[l]), row(conv_ln_g[l]),
            row(conv_ln_b[l]), w_conv_out[l].astype(BF16), hgrn_lb_logits.astype(F32), row(hgrn_norm_g[l]),
            w_hgrn_out[l].astype(BF16), w_out[l].astype(BF16), row(ln1_g[l]), row(ln1_b[l]),
            layer=l, alpha=alpha)
        x = _ffn(
            x, w_ffn_in[l].astype(BF16), w_ffn_dw[l].astype(F32), row(b_ffn_dw[l]),
            w_ffn_out[l].astype(BF16), row(ln2_g[l]), row(ln2_b[l]), alpha=alpha)
    return x
```

```python
import functools

import numpy as np
import jax
import jax.numpy as jnp
from jax import lax
from jax.experimental import pallas as pl
from jax.experimental.pallas import tpu as pltpu

F32 = jnp.float32
BF16 = jnp.bfloat16

CONV_DIM = 512
CONV_KERNEL = 31
HGRN_DIM = 1024
HGRN_HEADS = 8
HEAD_DIM = HGRN_DIM // HGRN_HEADS
FFN_KERNEL = 3
LN_EPS = 1e-5
RMS_EPS = 1e-6

OFF_CVAL = 0
OFF_CGATE = CONV_DIM
OFF_Q = 2 * CONV_DIM
OFF_F = OFF_Q + HGRN_DIM
OFF_I = OFF_F + HGRN_DIM
OFF_G = OFF_I + HGRN_DIM
OFF_M = OFF_G + HGRN_DIM

SUBLANES = 8
BF16_ROWS = 16
TOKEN_TILE = 256
CHUNK = 128
ROW_BLOCK = 16
ROW_CHUNK = 32
PROJ_COL_GROUP = 1024
FFN_COL_BLOCK = 256
GELU_C0 = 0.7978845608028654
GELU_C1 = 0.044715
CONV_HIST = 32
FFN_HIST = 8
LEVEL_DIAG = 100
VMEM_LIMIT_BYTES = 56 * 1024 * 1024


def _sig(x):
    return 0.5 * jnp.tanh(0.5 * x) + 0.5


def _dot(a, b):
    return jnp.dot(a, b, preferred_element_type=F32)


def _dot_nt(a, b):
    return lax.dot_general(a, b, (((1,), (1,)), ((), ())), preferred_element_type=F32)


def _layer_norm(x, g, b):
    mu = jnp.mean(x, axis=-1, keepdims=True)
    xc = x - mu
    var = jnp.mean(xc * xc, axis=-1, keepdims=True)
    return xc * lax.rsqrt(var + LN_EPS) * g + b


def _residual_layer_norm(x_ref, y_ref, g_ref, b_ref, o_ref, alpha):
    for r in range(0, x_ref.shape[0], ROW_CHUNK):
        rows = slice(r, r + ROW_CHUNK)
        o_ref[rows, :] = _layer_norm(alpha * x_ref[rows, :] + y_ref[rows, :], g_ref[...], b_ref[...])


def _run(items):
    for item in items:
        item()


def _levels(chunk):
    out, m = [], chunk // 2
    while m >= 1:
        out.append(m)
        m //= 2
    return out


def _level_matrix(chunk):
    t = np.arange(chunk)[:, None]
    s = np.arange(chunk)[None, :]
    hb = np.floor(np.log2(np.maximum(t ^ s, 1))).astype(np.int64)
    li = int(np.log2(chunk)) - 1 - hb
    return np.where(t > s, li, np.where(t == s, LEVEL_DIAG, -1)).astype(np.int32)


def _mixer_kernel(x_ref, w_in_ref, wdw_ref, bdw_ref, clg_ref, clb_ref, wco_ref, lbl_ref, hng_ref,
                  who_ref, wout_ref, l1g_ref, l1b_ref, tril_ref, lvl_ref,
                  o_ref,
                  xb_ref, proj_ref, cbuf_ref, csh_ref, cn_ref, yc_ref, ls_ref, qf_ref, kk_ref, b_ref, a_ref,
                  qi_ref, kt_ref, qb_ref, kb_ref, vb_ref, on_ref, og_ref, yh_ref, mx_ref, st_ref,
                  *, layer, alpha):
    j = pl.program_id(1)
    ts = x_ref.shape[0]
    chunk = b_ref.shape[1]
    levels = _levels(chunk)
    n_lev = len(levels)
    n_chunks = ts // chunk
    d_model = yc_ref.shape[1]

    @pl.when(j == 0)
    def _():
        cbuf_ref[0:CONV_HIST, :] = jnp.zeros((CONV_HIST, CONV_DIM), F32)
        st_ref[...] = jnp.zeros(st_ref.shape, F32)

    @pl.when(j != 0)
    def _():
        cbuf_ref[0:CONV_HIST, :] = cbuf_ref[ts:ts + CONV_HIST, :]

    sub = lax.broadcasted_iota(jnp.int32, (SUBLANES, HGRN_DIM), 0)
    xb_ref[...] = x_ref[...].astype(BF16)

    def proj(off):
        proj_ref[:, off:off + PROJ_COL_GROUP] = _dot(xb_ref[...], w_in_ref[:, off:off + PROJ_COL_GROUP])

    def glu():
        cbuf_ref[CONV_HIST:CONV_HIST + ts, :] = (
            proj_ref[:, OFF_CVAL:OFF_CVAL + CONV_DIM] * _sig(proj_ref[:, OFF_CGATE:OFF_CGATE + CONV_DIM]))

    def shift_copies():
        n_sh = csh_ref.shape[1]
        for r in range(1, SUBLANES):
            csh_ref[r - 1] = cbuf_ref[r:r + n_sh, :]

    def conv_block(t0):
        acc = jnp.broadcast_to(bdw_ref[...], (ROW_BLOCK, CONV_DIM))
        for k in range(CONV_KERNEL):
            a, r = divmod(k + CONV_HIST - (CONV_KERNEL - 1), SUBLANES)
            start = t0 + SUBLANES * a
            if r == 0:
                win = cbuf_ref[start:start + ROW_BLOCK, :]
            else:
                win = csh_ref[r - 1, start:start + ROW_BLOCK, :]
            acc = acc + wdw_ref[k:k + 1, :] * win
        y = _layer_norm(acc, clg_ref[...], clb_ref[...])
        cn_ref[t0:t0 + ROW_BLOCK, :] = (y * _sig(y)).astype(BF16)

    def conv_out():
        yc_ref[...] = _dot(cn_ref[...], wco_ref[...])

    lg = lbl_ref[...]
    e = jnp.exp(lg - jnp.max(lg, axis=0, keepdims=True))
    lb = jnp.sum(e[0:layer + 1], axis=0, keepdims=True) / jnp.sum(e, axis=0, keepdims=True)
    oml = 1.0 - lb

    def prep_block(t0):
        rows = slice(t0, t0 + ROW_CHUNK)
        th = jnp.tanh(0.5 * proj_ref[rows, OFF_F:OFF_F + HGRN_DIM])
        f = lb + oml * (0.5 + 0.5 * th)
        kk_ref[rows, :] = oml * (0.5 - 0.5 * th)
        logf = jnp.log(f)
        hi = logf.astype(BF16)
        r1 = logf - hi.astype(F32)
        mid = r1.astype(BF16)
        ls_ref[0, rows, :] = hi
        ls_ref[1, rows, :] = mid
        ls_ref[2, rows, :] = (r1 - mid.astype(F32)).astype(BF16)
        q = proj_ref[rows, OFF_Q:OFF_Q + HGRN_DIM]
        qf_ref[rows, :] = q * _sig(q)

    def cumsum(c):
        rows = slice(c * chunk, (c + 1) * chunk)
        tril = tril_ref[...]
        b_ref[c] = (_dot(tril, ls_ref[0, rows, :]) + _dot(tril, ls_ref[1, rows, :])
                    + _dot(tril, ls_ref[2, rows, :]))

    def level_operand(c, li):
        m = levels[li]
        r0 = c * chunk
        bc_ref, ac_ref = b_ref.at[c], a_ref.at[c]
        if m >= SUBLANES:
            for p in range(chunk // (2 * m)):
                base = p * 2 * m
                bref = bc_ref[base + m - 1:base + m, :]
                left = kk_ref[r0 + base:r0 + base + m, :] * jnp.exp(bref - bc_ref[base:base + m, :])
                right = (qf_ref[r0 + base + m:r0 + base + 2 * m, :]
                         * jnp.exp(bc_ref[base + m:base + 2 * m, :] - bref))
                ac_ref[li, base:base + 2 * m, :] = jnp.concatenate([left, right], axis=0).astype(BF16)
        elif m > 1:
            is_right = (sub & m) != 0
            for g2 in range(chunk // BF16_ROWS):
                parts = []
                for g in (2 * g2, 2 * g2 + 1):
                    gb = g * SUBLANES
                    if m == 4:
                        bref = jnp.broadcast_to(bc_ref[gb + 3:gb + 4, :], (SUBLANES, HGRN_DIM))
                    else:
                        bref = jnp.where(
                            sub < 4,
                            jnp.broadcast_to(bc_ref[gb + 1:gb + 2, :], (SUBLANES, HGRN_DIM)),
                            jnp.broadcast_to(bc_ref[gb + 5:gb + 6, :], (SUBLANES, HGRN_DIM)))
                    d = bc_ref[gb:gb + SUBLANES, :] - bref
                    ex = jnp.exp(jnp.where(is_right, d, -d))
                    src = jnp.where(is_right, qf_ref[r0 + gb:r0 + gb + SUBLANES, :],
                                    kk_ref[r0 + gb:r0 + gb + SUBLANES, :])
                    parts.append(src * ex)
                ac_ref[li, g2 * BF16_ROWS:(g2 + 1) * BF16_ROWS, :] = (
                    jnp.concatenate(parts, axis=0).astype(BF16))
        else:
            is_right = (sub & 1) != 0
            for g2 in range(chunk // BF16_ROWS):
                parts = []
                for g in (2 * g2, 2 * g2 + 1):
                    gb = g * SUBLANES
                    kk = kk_ref[r0 + gb:r0 + gb + SUBLANES, :]
                    parts.append(jnp.where(is_right, qf_ref[r0 + gb:r0 + gb + SUBLANES, :] * (1.0 - kk), kk))
                ac_ref[li, g2 * BF16_ROWS:(g2 + 1) * BF16_ROWS, :] = (
                    jnp.concatenate(parts, axis=0).astype(BF16))

    def inter_operands(c):
        r0 = c * chunk
        bc_ref = b_ref.at[c]
        b_last = bc_ref[chunk - 1:chunk, :]
        for rb in range(chunk // BF16_ROWS):
            rs = slice(rb * BF16_ROWS, (rb + 1) * BF16_ROWS)
            rd = slice(r0 + rb * BF16_ROWS, r0 + (rb + 1) * BF16_ROWS)
            bb = bc_ref[rs, :]
            qv = qf_ref[rd, :]
            kv = kk_ref[rd, :]
            qi_ref[c, rs, :] = (qv * jnp.exp(bb)).astype(BF16)
            kt_ref[c, rs, :] = (kv * jnp.exp(b_last - bb)).astype(BF16)
            qb_ref[c, rs, :] = qv.astype(BF16)
            kb_ref[c, rs, :] = kv.astype(BF16)
            vb_ref[c, rs, :] = proj_ref[rd, OFF_I:OFF_I + HGRN_DIM].astype(BF16)

    def head(c, h):
        rows = slice(c * chunk, (c + 1) * chunk)
        hs = slice(h * HEAD_DIM, (h + 1) * HEAD_DIM)
        lvl = lvl_ref[...]
        sc = jnp.where(lvl == LEVEL_DIAG, _dot_nt(qb_ref[c, :, hs], kb_ref[c, :, hs]), 0.0)
        for li in range(n_lev):
            a = a_ref[c, li, :, hs]
            sc = jnp.where(lvl == li, _dot_nt(a, a), sc)
        st = st_ref[h]
        o = _dot(sc.astype(BF16), vb_ref[c, :, hs]) + _dot_nt(qi_ref[c, :, hs], st.astype(BF16))
        v_t = proj_ref[rows, OFF_I + h * HEAD_DIM:OFF_I + (h + 1) * HEAD_DIM].T.astype(BF16)
        eb_last = jnp.exp(b_ref[c, chunk - 1:chunk, hs])
        st_ref[h] = st * eb_last + _dot(v_t, kt_ref[c, :, hs])
        ms = jnp.mean(o * o, axis=-1, keepdims=True)
        on_ref[rows, hs] = o * lax.rsqrt(ms + RMS_EPS)

    def gate_block(t0):
        rows = slice(t0, t0 + ROW_CHUNK)
        gz = proj_ref[rows, OFF_G:OFF_G + HGRN_DIM]
        og_ref[rows, :] = (on_ref[rows, :] * hng_ref[...] * (gz * _sig(gz))).astype(BF16)

    def merge_block(t0):
        rows = slice(t0, t0 + ROW_CHUNK)
        ga = _sig(proj_ref[rows, OFF_M:OFF_M + d_model])
        gb = _sig(proj_ref[rows, OFF_M + d_model:OFF_M + 2 * d_model])
        mx_ref[rows, :] = (ga * yc_ref[rows, :] + gb * yh_ref[rows, :]).astype(BF16)

    conv_blocks = [functools.partial(conv_block, t0) for t0 in range(0, ts, ROW_BLOCK)]
    prep_blocks = [functools.partial(prep_block, t0) for t0 in range(0, ts, ROW_CHUNK)]
    gate_blocks = [functools.partial(gate_block, t0) for t0 in range(0, ts, ROW_CHUNK)]
    quarter = len(conv_blocks) // 4
    half_prep = len(prep_blocks) // 2
    gates_per_chunk = chunk // ROW_CHUNK

    proj(OFF_CVAL)
    glu()
    proj(OFF_F)
    shift_copies()
    proj(OFF_Q)
    _run(conv_blocks[0:quarter])
    proj(OFF_I)
    _run(prep_blocks[0:half_prep])
    _run(conv_blocks[quarter:2 * quarter])
    proj(OFF_G)
    _run(prep_blocks[half_prep:])
    _run(conv_blocks[2 * quarter:3 * quarter])
    cumsum(0)
    proj(OFF_M)
    _run(conv_blocks[3 * quarter:])
    for li in range(n_lev):
        level_operand(0, li)
    inter_operands(0)
    proj(OFF_M + d_model)
    conv_out()
    for c in range(n_chunks):
        side = []
        if c + 1 < n_chunks:
            cumsum(c + 1)
            side += [functools.partial(level_operand, c + 1, li) for li in range(n_lev)]
            side.append(functools.partial(inter_operands, c + 1))
        if c > 0:
            side += gate_blocks[(c - 1) * gates_per_chunk:c * gates_per_chunk]
        for h in range(HGRN_HEADS):
            head(c, h)
            _run(side[h * len(side) // HGRN_HEADS:(h + 1) * len(side) // HGRN_HEADS])
    _run(gate_blocks[(n_chunks - 1) * gates_per_chunk:])
    yh_ref[...] = _dot(og_ref[...], who_ref[...])
    for t0 in range(0, ts, ROW_CHUNK):
        merge_block(t0)
    yh_ref[...] = _dot(mx_ref[...], wout_ref[...])
    _residual_layer_norm(x_ref, yh_ref, l1g_ref, l1b_ref, o_ref, alpha)


def _resident(shape):
    nd = len(shape)
    return pl.BlockSpec(shape, lambda b, j: (0,) * nd, pipeline_mode=pl.Buffered(1))


def _mixer(x, w_in, wdw, bdw, clg, clb, wco, lbl, hng, who, wout, l1g, l1b, *, layer, alpha):
    bsz, seq, d_model = x.shape
    ts, chunk = TOKEN_TILE, CHUNK
    assert seq % ts == 0 and ts % chunk == 0 and d_model == HGRN_DIM
    assert w_in.shape[1] % PROJ_COL_GROUP == 0 and ts % ROW_CHUNK == 0
    n_lev = len(_levels(chunk))
    n_chunks = ts // chunk
    tril = jnp.asarray(np.tril(np.ones((chunk, chunk), np.float32)), BF16)
    lvl = jnp.asarray(_level_matrix(chunk))
    consts = (w_in, wdw, bdw, clg, clb, wco, lbl, hng, who, wout, l1g, l1b, tril, lvl)
    tile = pl.BlockSpec((None, ts, d_model), lambda b, j: (b, j, 0))
    scratch = [
        pltpu.VMEM((ts, d_model), BF16),
        pltpu.VMEM((ts, w_in.shape[1]), F32),
        pltpu.VMEM((ts + CONV_HIST, CONV_DIM), F32),
        pltpu.VMEM((SUBLANES - 1, ts + CONV_HIST - SUBLANES, CONV_DIM), F32),
        pltpu.VMEM((ts, CONV_DIM), BF16),
        pltpu.VMEM((ts, d_model), F32),
        pltpu.VMEM((3, ts, HGRN_DIM), BF16),
        pltpu.VMEM((ts, HGRN_DIM), F32),
        pltpu.VMEM((ts, HGRN_DIM), F32),
        pltpu.VMEM((n_chunks, chunk, HGRN_DIM), F32),
        pltpu.VMEM((n_chunks, n_lev, chunk, HGRN_DIM), BF16),
        pltpu.VMEM((n_chunks, chunk, HGRN_DIM), BF16),
        pltpu.VMEM((n_chunks, chunk, HGRN_DIM), BF16),
        pltpu.VMEM((n_chunks, chunk, HGRN_DIM), BF16),
        pltpu.VMEM((n_chunks, chunk, HGRN_DIM), BF16),
        pltpu.VMEM((n_chunks, chunk, HGRN_DIM), BF16),
        pltpu.VMEM((ts, HGRN_DIM), F32),
        pltpu.VMEM((ts, HGRN_DIM), BF16),
        pltpu.VMEM((ts, d_model), F32),
        pltpu.VMEM((ts, d_model), BF16),
        pltpu.VMEM((HGRN_HEADS, HEAD_DIM, HEAD_DIM), F32),
    ]
    return pl.pallas_call(
        functools.partial(_mixer_kernel, layer=layer, alpha=alpha),
        grid=(bsz, seq // ts),
        in_specs=[tile] + [_resident(c.shape) for c in consts],
        out_specs=tile,
        out_shape=jax.ShapeDtypeStruct(x.shape, x.dtype),
        scratch_shapes=scratch,
        compiler_params=pltpu.CompilerParams(
            dimension_semantics=("arbitrary", "arbitrary"), vmem_limit_bytes=VMEM_LIMIT_BYTES),
        name="mixer",
    )(x, *consts)


def _ffn_kernel(x_ref, wfi_ref, wdw_ref, bdw_ref, wfo_ref, g_ref, b_ref, o_ref,
                xb_ref, ubuf_ref, gv_ref, h_ref, y_ref, *, alpha):
    j = pl.program_id(1)
    ts = x_ref.shape[0]
    d_ff = gv_ref.shape[1]

    @pl.when(j == 0)
    def _():
        ubuf_ref[0:FFN_HIST, :] = jnp.zeros((FFN_HIST, d_ff), F32)

    @pl.when(j != 0)
    def _():
        ubuf_ref[0:FFN_HIST, :] = ubuf_ref[ts:ts + FFN_HIST, :]

    xb_ref[...] = x_ref[...].astype(BF16)
    for n in range(d_ff // FFN_COL_BLOCK):
        cs = slice(n * FFN_COL_BLOCK, (n + 1) * FFN_COL_BLOCK)
        ubuf_ref[FFN_HIST:FFN_HIST + ts, cs] = _dot(xb_ref[...], wfi_ref[:, cs])
        gv_ref[:, cs] = _dot(xb_ref[...], wfi_ref[:, d_ff + n * FFN_COL_BLOCK:d_ff + (n + 1) * FFN_COL_BLOCK])
        for r in range(0, ts, ROW_CHUNK):
            u = bdw_ref[:, cs]
            for k in range(FFN_KERNEL):
                lo = FFN_HIST - (FFN_KERNEL - 1) + k + r
                u = u + wdw_ref[k:k + 1, cs] * ubuf_ref[lo:lo + ROW_CHUNK, cs]
            z = u * (GELU_C0 + (GELU_C0 * GELU_C1) * (u * u))
            hu = 0.5 * u
            h_ref[r:r + ROW_CHUNK, cs] = ((hu + hu * jnp.tanh(z)) * gv_ref[r:r + ROW_CHUNK, cs]).astype(BF16)
    y_ref[...] = _dot(h_ref[...], wfo_ref[...])
    _residual_layer_norm(x_ref, y_ref, g_ref, b_ref, o_ref, alpha)


def _ffn(x, wfi, wdw, bdw, wfo, g, b, *, alpha):
    bsz, seq, d_model = x.shape
    ts = TOKEN_TILE
    d_ff = wfo.shape[0]
    assert seq % ts == 0
    consts = (wfi, wdw, bdw, wfo, g, b)
    tile = pl.BlockSpec((None, ts, d_model), lambda b_, j: (b_, j, 0))
    assert d_ff % FFN_COL_BLOCK == 0 and ts % ROW_CHUNK == 0
    scratch = [
        pltpu.VMEM((ts, d_model), BF16),
        pltpu.VMEM((ts + FFN_HIST, d_ff), F32),
        pltpu.VMEM((ts, d_ff), F32),
        pltpu.VMEM((ts, d_ff), BF16),
        pltpu.VMEM((ts, d_model), F32),
    ]
    return pl.pallas_call(
        functools.partial(_ffn_kernel, alpha=alpha),
        grid=(bsz, seq // ts),
        in_specs=[tile] + [_resident(c.shape) for c in consts],
        out_specs=tile,
        out_shape=jax.ShapeDtypeStruct(x.shape, x.dtype),
        scratch_shapes=scratch,
        compiler_params=pltpu.CompilerParams(
            dimension_semantics=("arbitrary", "arbitrary"), vmem_limit_bytes=VMEM_LIMIT_BYTES),
        name="ffn",
    )(x, *consts)


def kernel(x, w_in, w_conv_dw, b_conv_dw, conv_ln_g, conv_ln_b, w_conv_out, hgrn_lb_logits, hgrn_norm_g,
           w_hgrn_out, w_out, ln1_g, ln1_b, w_ffn_in, w_ffn_dw, b_ffn_dw, w_ffn_out, ln2_g, ln2_b):
    depth = w_in.shape[0]
    alpha = (2.0 * depth) ** 0.25
    row = lambda v: v.reshape(1, -1).astype(F32)
    for l in range(depth):
        x = _mixer(
            x, w_in[l].astype(BF16), w_conv_dw[l].astype(F32), row(b_conv_dw[l]), row(conv_ln_g[l]),
            row(conv_ln_b[l]), w_conv_out[l].astype(BF16), hgrn_lb_logits.astype(F32), row(hgrn_norm_g[l]),
            w_hgrn_out[l].astype(BF16), w_out[l].astype(BF16), row(ln1_g[l]), row(ln1_b[l]),
            layer=l, alpha=alpha)
        x = _ffn(
            x, w_ffn_in[l].astype(BF16), w_ffn_dw[l].astype(F32), row(b_ffn_dw[l]),
            w_ffn_out[l].astype(BF16), row(ln2_g[l]), row(ln2_b[l]), alpha=alpha)
    return x
```
